```python
import math
import jax, jax.numpy as jnp
from jax import lax
import numpy as np

D_MODEL = 1024
BATCH = 4
SEQ = 8192
DEPTH = 1

HEAD_DIM = D_MODEL // 16
N_HEADS_SGU = 8
N_HEADS_FOX = 8
D_SGU = N_HEADS_SGU * HEAD_DIM
D_FOX = N_HEADS_FOX * HEAD_DIM
D_MIX = D_SGU + D_FOX
D_IN_PROJ = 2 * D_SGU + 3 * D_FOX + N_HEADS_FOX
CHUNK = 128
Q_BLOCK = 128
N_GROUPS = 4
EXPERTS_PER_GROUP = 4
N_EXPERTS = N_GROUPS * EXPERTS_PER_GROUP
TOP_K_INNER = 2
D_EXPERT = D_MODEL // 2
N_ADA = 6
EPS = 1e-6

kernel_name = "hybrid_sgu_fox_hmoe_adaln"


def rms_norm(x, g):
    xf = x.astype(jnp.float32)
    y = xf * lax.rsqrt(jnp.mean(xf * xf, axis=-1, keepdims=True) + EPS)
    return (y * g.astype(jnp.float32)).astype(x.dtype)


def sgu_mixer(u, v, g_sgu, w_spatial, b_spatial):
    B, S, _ = u.shape
    u = jax.nn.gelu(u)
    v = jax.nn.gelu(v).reshape(B, S // CHUNK, CHUNK, N_HEADS_SGU, HEAD_DIM)
    v = rms_norm(v, g_sgu.reshape(N_HEADS_SGU, HEAD_DIM))
    mask = jnp.tril(jnp.ones((CHUNK, CHUNK), dtype=bool))
    ws = jnp.where(mask[None], w_spatial, jnp.zeros((), w_spatial.dtype)).astype(v.dtype)
    z = jnp.einsum('hts,bnshd->bnthd', ws, v) + b_spatial.T[None, None, :, :, None].astype(v.dtype)
    return u * z.reshape(B, S, D_SGU)


def fox_attention(q, k, v, f_logit, b_forget):
    B, S, H, hd = q.shape
    log_f = jax.nn.log_sigmoid(f_logit.astype(jnp.float32) + b_forget.astype(jnp.float32))
    F = jnp.cumsum(log_f, axis=1).transpose(0, 2, 1)
    q = q.transpose(0, 2, 1, 3)
    k = k.transpose(0, 2, 1, 3)
    v = v.transpose(0, 2, 1, 3)
    scale = 1.0 / math.sqrt(hd)
    kpos = jnp.arange(S)

    def block(i):
        start = i * Q_BLOCK
        qi = lax.dynamic_slice_in_dim(q, start, Q_BLOCK, axis=2)
        Fi = lax.dynamic_slice_in_dim(F, start, Q_BLOCK, axis=2)
        s = jnp.einsum('bhqd,bhkd->bhqk', qi, k, preferred_element_type=jnp.float32) * scale
        s = s + Fi[..., :, None] - F[..., None, :]
        qpos = start + jnp.arange(Q_BLOCK)
        s = jnp.where(kpos[None, :] <= qpos[:, None], s, -jnp.inf)
        p = jax.nn.softmax(s, axis=-1).astype(v.dtype)
        return jnp.einsum('bhqk,bhkd->bhqd', p, v)

    out = lax.map(block, jnp.arange(S // Q_BLOCK))
    return out.transpose(1, 0, 3, 2, 4).reshape(B, S, H, hd)


def hybrid_mixer(h, w_in, g_sgu, w_spatial, b_spatial, b_forget, g_out_sgu, g_out_fox, w_out):
    B, S, _ = h.shape
    proj = h @ w_in
    splits = np.cumsum([D_SGU, D_SGU, D_FOX, D_FOX, D_FOX]).tolist()
    u, vs, q, k, vf, fl = jnp.split(proj, splits, axis=-1)
    y_sgu = sgu_mixer(u, vs, g_sgu, w_spatial, b_spatial)
    shp = (B, S, N_HEADS_FOX, HEAD_DIM)
    y_fox = fox_attention(q.reshape(shp), k.reshape(shp), vf.reshape(shp), fl, b_forget).reshape(B, S, D_FOX)
    y = jnp.concatenate([rms_norm(y_sgu, g_out_sgu), rms_norm(y_fox, g_out_fox)], axis=-1)
    return y @ w_out


def hier_moe(h, w_rg, b_rg, w_re, b_re, w_gate, w_up, w_down):
    B, S, D = h.shape
    t = h.reshape(-1, D)
    lg = (t @ w_rg + b_rg).astype(jnp.float32)
    p_group = jax.nn.softmax(lg, axis=-1)
    g_sel = jnp.argmax(lg, axis=-1)
    p_sel = jnp.take_along_axis(p_group, g_sel[:, None], axis=1)
    le = (jnp.einsum('nd,gde->nge', t, w_re) + b_re).astype(jnp.float32)
    le_sel = jnp.take_along_axis(le, g_sel[:, None, None], axis=1)[:, 0]
    top_v, top_i = lax.top_k(le_sel, TOP_K_INNER)
    w_k = p_sel * jax.nn.softmax(top_v, axis=-1)
    expert_idx = g_sel[:, None] * EXPERTS_PER_GROUP + top_i
    combine = jnp.sum(jax.nn.one_hot(expert_idx, N_EXPERTS, dtype=jnp.float32) * w_k[..., None],
                      axis=1).astype(h.dtype)
    y = jnp.zeros_like(t)
    for e in range(N_EXPERTS):
        a = jax.nn.silu(t @ w_gate[e]) * (t @ w_up[e])
        y = y + combine[:, e:e + 1] * (a @ w_down[e])
    return y.reshape(B, S, D)


def setup_inputs(seed: int = 0) -> dict:
    key = jax.random.key(seed)
    ks = jax.random.split(key, 24)
    n = lambda k, shape, s: jax.random.normal(k, shape, jnp.float32) * s
    L = DEPTH
    return {
        "x": n(ks[0], (BATCH, SEQ, D_MODEL), 1.0),
        "c": n(ks[1], (BATCH, D_MODEL), 1.0),
        "w_ada": n(ks[2], (L, D_MODEL, N_ADA * D_MODEL), 0.02),
        "b_ada": n(ks[3], (L, N_ADA * D_MODEL), 0.02),
        "g_norm_mix": 1.0 + n(ks[4], (L, D_MODEL), 0.02),
        "w_in": n(ks[5], (L, D_MODEL, D_IN_PROJ), D_MODEL ** -0.5),
        "g_sgu": 1.0 + n(ks[6], (L, D_SGU), 0.02),
        "w_spatial": n(ks[7], (L, N_HEADS_SGU, CHUNK, CHUNK), 0.5 * CHUNK ** -0.5),
        "b_spatial": 1.0 + n(ks[8], (L, N_HEADS_SGU, CHUNK), 0.1),
        "b_forget": 3.0 + n(ks[9], (L, N_HEADS_FOX), 0.1),
        "g_out_sgu": 1.0 + n(ks[10], (L, D_SGU), 0.02),
        "g_out_fox": 1.0 + n(ks[11], (L, D_FOX), 0.02),
        "w_out": n(ks[12], (L, D_MIX, D_MODEL), D_MIX ** -0.5),
        "g_norm_ffn": 1.0 + n(ks[13], (L, D_MODEL), 0.02),
        "w_router_group": n(ks[14], (L, D_MODEL, N_GROUPS), D_MODEL ** -0.5),
        "b_router_group": n(ks[15], (L, N_GROUPS), 0.01),
        "w_router_expert": n(ks[16], (L, N_GROUPS, D_MODEL, EXPERTS_PER_GROUP), D_MODEL ** -0.5),
        "b_router_expert": n(ks[17], (L, N_GROUPS, EXPERTS_PER_GROUP), 0.01),
        "w_gate": n(ks[18], (L, N_EXPERTS, D_MODEL, D_EXPERT), D_MODEL ** -0.5),
        "w_up": n(ks[19], (L, N_EXPERTS, D_MODEL, D_EXPERT), D_MODEL ** -0.5),
        "w_down": n(ks[20], (L, N_EXPERTS, D_EXPERT, D_MODEL), D_EXPERT ** -0.5),
        "g_final": 1.0 + n(ks[21], (D_MODEL,), 0.02),
    }


def reference(x, c, w_ada, b_ada, g_norm_mix, w_in, g_sgu, w_spatial, b_spatial, b_forget,
              g_out_sgu, g_out_fox, w_out, g_norm_ffn, w_router_group, b_router_group,
              w_router_expert, b_router_expert, w_gate, w_up, w_down, g_final):
    c_act = jax.nn.silu(c)
    for l in range(DEPTH):
        ada = (c_act @ w_ada[l] + b_ada[l])[:, None, :]
        sh1, sc1, gt1, sh2, sc2, gt2 = jnp.split(ada, N_ADA, axis=-1)
        h = rms_norm(x, g_norm_mix[l]) * (1 + sc1) + sh1
        x = x + gt1 * hybrid_mixer(h, w_in[l], g_sgu[l], w_spatial[l], b_spatial[l], b_forget[l],
                                   g_out_sgu[l], g_out_fox[l], w_out[l])
        h = rms_norm(x, g_norm_ffn[l]) * (1 + sc2) + sh2
        x = x + gt2 * hier_moe(h, w_router_group[l], b_router_group[l], w_router_expert[l],
                               b_router_expert[l], w_gate[l], w_up[l], w_down[l])
    return rms_norm(x, g_final)
```

```python
import functools
import math

import jax
import jax.numpy as jnp
from jax import lax
from jax.experimental import pallas as pl
from jax.experimental.pallas import tpu as pltpu

F32 = jnp.float32
BF16 = jnp.bfloat16

HEAD_DIM = 64
N_HEADS_SGU = 8
N_HEADS_FOX = 8
D_SGU = N_HEADS_SGU * HEAD_DIM
D_FOX = N_HEADS_FOX * HEAD_DIM
CHUNK = 128
N_GROUPS = 4
EXPERTS_PER_GROUP = 4
N_PAIRS = 6
N_CLASSES = N_GROUPS * N_PAIRS
CLASS_ROWS = 32
EPS = 1e-6
LOG2E = math.log2(math.e)
LANES = 128
PAY = LANES
F_ROWS = 16

VMEM_LIMIT = 56 * 1024 * 1024

PAIR_A = (0, 0, 0, 1, 1, 2)
PAIR_B = (1, 2, 3, 2, 3, 3)


def _cparams(sem):
    return pltpu.CompilerParams(dimension_semantics=sem, vmem_limit_bytes=VMEM_LIMIT)


def _ada_kernel(c_ref, w_ref, b_ref, o_ref):
    c = c_ref[...]
    ca = (c * jax.nn.sigmoid(c)).astype(BF16)
    o_ref[...] = jnp.dot(ca, w_ref[...].astype(BF16), preferred_element_type=F32) + b_ref[...]


def _ada(c, w_ada, b_ada):
    B, D = c.shape
    n6 = w_ada.shape[1]
    rows = 8
    cp = jnp.zeros((rows, D), F32).at[:B].set(c)
    tn = 1536
    out = pl.pallas_call(
        _ada_kernel,
        out_shape=jax.ShapeDtypeStruct((rows, n6), F32),
        grid=(n6 // tn,),
        in_specs=[pl.BlockSpec((rows, D), lambda j: (0, 0)),
                  pl.BlockSpec((D, tn), lambda j: (0, j)),
                  pl.BlockSpec((1, tn), lambda j: (0, j))],
        out_specs=pl.BlockSpec((rows, tn), lambda j: (0, j)),
        compiler_params=_cparams(("arbitrary",)),
        name="ada",
    )(cp, w_ada, b_ada.reshape(1, n6))
    return out[:B].reshape(B, 6, D)


def _log_sigmoid(x):
    return jnp.minimum(x, 0.0) - jnp.log1p(jnp.exp(-jnp.abs(x)))


def _inproj_kernel(x_ref, ada_ref, gmix_ref, wt_ref, wn_ref, gsgu_ref, ws_ref, bsp_ref, bf_ref, gos_ref,
                   q_ref, kT_ref, v_ref, ysT_ref, F_ref, carry_ref, *, tm):
    j = pl.program_id(1)
    x = x_ref[0]
    ada = ada_ref[0]
    sh1, sc1 = ada[0:1], ada[1:2]
    ms = jnp.mean(x * x, axis=-1, keepdims=True)
    h = (x * lax.rsqrt(ms + EPS) * gmix_ref[...]) * (1.0 + sc1) + sh1
    hb = h.astype(BF16)
    pT = lax.dot_general(wt_ref[...], hb, (((1,), (1,)), ((), ())), preferred_element_type=F32)
    pN = jnp.dot(hb, wn_ref[...], preferred_element_type=F32)
    q_ref[0] = (pN[:, :D_FOX] * (LOG2E / math.sqrt(HEAD_DIM))).astype(BF16)
    v_ref[0] = pN[:, D_FOX:].astype(BF16)
    kT_ref[0] = pT[2 * D_SGU:2 * D_SGU + D_FOX].astype(BF16)

    uT = jax.nn.gelu(pT[0:D_SGU])
    v3 = jax.nn.gelu(pT[D_SGU:2 * D_SGU]).reshape(N_HEADS_SGU, HEAD_DIM, tm)
    msv = jnp.mean(v3 * v3, axis=1, keepdims=True)
    vn = ((v3 * lax.rsqrt(msv + EPS)).reshape(D_SGU, tm) * gsgu_ref[...]).astype(BF16)
    srow = lax.broadcasted_iota(jnp.int32, (CHUNK, CHUNK), 0)
    tcol = lax.broadcasted_iota(jnp.int32, (CHUNK, CHUNK), 1)
    causal = srow <= tcol
    rows = []
    for hd in range(N_HEADS_SGU):
        w = jnp.where(causal, ws_ref[hd], jnp.zeros((), BF16))
        cols = [jnp.dot(vn[hd * HEAD_DIM:(hd + 1) * HEAD_DIM, c * CHUNK:(c + 1) * CHUNK], w,
                        preferred_element_type=F32) + bsp_ref[hd * HEAD_DIM:(hd + 1) * HEAD_DIM, :]
                for c in range(tm // CHUNK)]
        rows.append(jnp.concatenate(cols, axis=1))
    y = uT * jnp.concatenate(rows, axis=0)
    msy = jnp.mean(y * y, axis=0, keepdims=True)
    ysT_ref[0] = (y * lax.rsqrt(msy + EPS) * gos_ref[...]).astype(BF16)

    lf = _log_sigmoid(pT[2 * D_SGU + D_FOX:2 * D_SGU + D_FOX + N_HEADS_FOX] + bf_ref[...])
    lane = lax.broadcasted_iota(jnp.int32, lf.shape, 1)
    sft = 1
    while sft < tm:
        lf = lf + jnp.where(lane >= sft, pltpu.roll(lf, sft, axis=1), 0.0)
        sft *= 2

    @pl.when(j == 0)
    def _():
        carry_ref[...] = jnp.zeros_like(carry_ref)

    Fc = lf + carry_ref[:, 0:1]
    carry_ref[...] = jnp.broadcast_to(Fc[:, tm - 1:tm], carry_ref.shape)
    F_ref[0] = Fc * LOG2E


def _inproj(x, ada, g_norm_mix, w_in, g_sgu, w_spatial, b_spatial, b_forget, g_out_sgu, tm):
    B, S, D = x.shape
    o_u, o_v, o_q, o_k, o_vf, o_f = 0, D_SGU, 2 * D_SGU, 2 * D_SGU + D_FOX, 2 * D_SGU + 2 * D_FOX, 2 * D_SGU + 3 * D_FOX
    wT = jnp.concatenate([w_in[:, o_u:o_q], w_in[:, o_k:o_vf], w_in[:, o_f:],
                          jnp.zeros((D, F_ROWS - N_HEADS_FOX), F32)], axis=1).T.astype(BF16)
    wN = jnp.concatenate([w_in[:, o_q:o_k], w_in[:, o_vf:o_f]], axis=1).astype(BF16)
    nT = wT.shape[0]
    wsT = jnp.swapaxes(w_spatial, 1, 2).astype(BF16)
    bsp = jnp.repeat(b_spatial, HEAD_DIM, axis=0)
    kern = functools.partial(_inproj_kernel, tm=tm)
    const = lambda shape: pl.BlockSpec(shape, lambda b, j: (0,) * len(shape))
    return pl.pallas_call(
        kern,
        out_shape=(jax.ShapeDtypeStruct((B, S, D_FOX), BF16),
                   jax.ShapeDtypeStruct((B, D_FOX, S), BF16),
                   jax.ShapeDtypeStruct((B, S, D_FOX), BF16),
                   jax.ShapeDtypeStruct((B, D_SGU, S), BF16),
                   jax.ShapeDtypeStruct((B, N_HEADS_FOX, S), F32)),
        grid=(B, S // tm),
        in_specs=[pl.BlockSpec((1, tm, D), lambda b, j: (b, j, 0)),
                  pl.BlockSpec((1, 6, D), lambda b, j: (b, 0, 0)),
                  const((1, D)), const((nT, D)), const((D, 2 * D_FOX)), const((D_SGU, 1)),
                  const((N_HEADS_SGU, CHUNK, CHUNK)), const((D_SGU, CHUNK)), const((N_HEADS_FOX, 1)),
                  const((D_SGU, 1))],
        out_specs=(pl.BlockSpec((1, tm, D_FOX), lambda b, j: (b, j, 0)),
                   pl.BlockSpec((1, D_FOX, tm), lambda b, j: (b, 0, j)),
                   pl.BlockSpec((1, tm, D_FOX), lambda b, j: (b, j, 0)),
                   pl.BlockSpec((1, D_SGU, tm), lambda b, j: (b, 0, j)),
                   pl.BlockSpec((1, N_HEADS_FOX, tm), lambda b, j: (b, 0, j))),
        scratch_shapes=[pltpu.VMEM((N_HEADS_FOX, LANES), F32)],
        compiler_params=_cparams(("arbitrary", "arbitrary")),
        name="inproj",
    )(x, ada, g_norm_mix.reshape(1, D), wT, wN, g_sgu.reshape(D_SGU, 1), wsT, bsp,
      b_forget.reshape(N_HEADS_FOX, 1), g_out_sgu.reshape(D_SGU, 1))


def _attn_kernel(q_ref, kT_ref, v_ref, F_ref, o_ref, m_scr, l_scr, acc_scr, *, tq, S):
    pair = pl.program_id(1)
    tk = tq
    rep = tk // LANES
    lane = lax.broadcasted_iota(jnp.int32, (tq, LANES), 1)
    row = lax.broadcasted_iota(jnp.int32, (tq, tk), 0)
    col = lax.broadcasted_iota(jnp.int32, (tq, tk), 1)

    def q_body(qi, carry):
        q = q_ref[0, pl.ds(pl.multiple_of(qi * tq, tq), tq), :]
        outs = []
        for hh in range(2):
            qh = jnp.where((lane >= hh * HEAD_DIM) & (lane < (hh + 1) * HEAD_DIM), q, jnp.zeros((), BF16))
            m_scr[...] = jnp.full(m_scr.shape, -jnp.inf, F32)
            l_scr[...] = jnp.zeros(l_scr.shape, F32)
            acc_scr[...] = jnp.zeros(acc_scr.shape, F32)

            def k_step(kj, diag):
                koff = pl.multiple_of(kj * tk, tk)
                s = jnp.dot(qh, kT_ref[0, :, pl.ds(koff, tk)], preferred_element_type=F32)
                s = s - F_ref[0, pl.ds(2 * pair + hh, 1), pl.ds(koff, tk)]
                if diag:
                    s = jnp.where(col <= row, s, -jnp.inf)
                m_prev = m_scr[...]
                m_new = jnp.maximum(m_prev, jnp.max(s, axis=1, keepdims=True))
                alpha = jnp.exp2(m_prev - m_new)
                p = jnp.exp2(s - jnp.concatenate([m_new] * rep, axis=1))
                l_scr[...] = alpha * l_scr[...] + jnp.sum(p, axis=1, keepdims=True)
                acc_scr[...] = alpha * acc_scr[...] + jnp.dot(
                    p.astype(BF16), v_ref[0, pl.ds(koff, tk), :], preferred_element_type=F32)
                m_scr[...] = m_new

            def k_body(kj, c):
                k_step(kj, False)
                return c

            lax.fori_loop(0, qi, k_body, 0)
            k_step(qi, True)
            outs.append(acc_scr[...] / l_scr[...])
        o = jnp.where(lane < HEAD_DIM, outs[0], outs[1])
        o_ref[0, pl.ds(pl.multiple_of(qi * tq, tq), tq), :] = o.astype(o_ref.dtype)
        return carry

    lax.fori_loop(0, S // tq, q_body, 0)


def _attention(q, kT, v, Fl2, tq):
    B, S, _ = q.shape
    kern = functools.partial(_attn_kernel, tq=tq, S=S)
    return pl.pallas_call(
        kern,
        out_shape=jax.ShapeDtypeStruct((B, S, D_FOX), BF16),
        grid=(B, N_HEADS_FOX // 2),
        in_specs=[pl.BlockSpec((1, S, LANES), lambda b, p: (b, 0, p)),
                  pl.BlockSpec((1, LANES, S), lambda b, p: (b, p, 0)),
                  pl.BlockSpec((1, S, LANES), lambda b, p: (b, 0, p)),
                  pl.BlockSpec((1, N_HEADS_FOX, S), lambda b, p: (b, 0, 0))],
        out_specs=pl.BlockSpec((1, S, LANES), lambda b, p: (b, 0, p)),
        scratch_shapes=[pltpu.VMEM((tq, LANES), F32), pltpu.VMEM((tq, LANES), F32),
                        pltpu.VMEM((tq, LANES), F32)],
        compiler_params=_cparams(("arbitrary", "arbitrary")),
        name="attn",
    )(q, kT, v, Fl2)


def _first_argmax(vals):
    best, idx = vals[0], jnp.zeros(vals[0].shape, jnp.int32)
    for i in range(1, len(vals)):
        upd = vals[i] > best
        idx = jnp.where(upd, i, idx)
        best = jnp.where(upd, vals[i], best)
    return best, idx


def _outproj_kernel(yf_ref, ysT_ref, x_ref, ada_ref, gof_ref, wof_ref, wos_ref, gffn_ref, wr_ref, br_ref,
                    tri_ref, x1_ref, haug_ref, cls_ref, rank_ref, hist_ref, *, tm, D):
    ada = ada_ref[0]
    gt1, sh2, sc2 = ada[2:3], ada[3:4], ada[4:5]
    yf = yf_ref[0].astype(F32)
    msf = jnp.mean(yf * yf, axis=-1, keepdims=True)
    yfn = (yf * lax.rsqrt(msf + EPS) * gof_ref[...]).astype(BF16)
    mix = jnp.dot(yfn, wof_ref[...], preferred_element_type=F32)
    mix = mix + lax.dot_general(ysT_ref[0], wos_ref[...], (((0,), (0,)), ((), ())),
                                preferred_element_type=F32)
    x1 = x_ref[0] + gt1 * mix
    x1_ref[0] = x1
    ms = jnp.mean(x1 * x1, axis=-1, keepdims=True)
    h2 = (x1 * lax.rsqrt(ms + EPS) * gffn_ref[...]) * (1.0 + sc2) + sh2
    haug_ref[0, :, 0:D] = h2

    lg = lax.dot_general(wr_ref[...], h2.astype(BF16), (((1,), (1,)), ((), ())),
                         preferred_element_type=F32) + br_ref[...]
    gl = [lg[i:i + 1] for i in range(N_GROUPS)]
    gmax, gsel = _first_argmax(gl)
    denom = gl[0] * 0.0
    for i in range(N_GROUPS):
        denom = denom + jnp.exp(gl[i] - gmax)
    p_sel = 1.0 / denom
    le = []
    for e in range(EXPERTS_PER_GROUP):
        r = lg[N_GROUPS + e:N_GROUPS + e + 1]
        for g in range(1, N_GROUPS):
            r = jnp.where(gsel == g, lg[N_GROUPS + g * EXPERTS_PER_GROUP + e:N_GROUPS + g * EXPERTS_PER_GROUP + e + 1], r)
        le.append(r)
    v1, i1 = _first_argmax(le)
    v2, i2 = _first_argmax([jnp.where(i1 == e, -jnp.inf, le[e]) for e in range(EXPERTS_PER_GROUP)])
    e21 = jnp.exp(v2 - v1)
    w1 = p_sel * (1.0 / (1.0 + e21))
    w2 = p_sel * (e21 / (1.0 + e21))
    first_low = i1 < i2
    ea = jnp.where(first_low, i1, i2)
    eb = jnp.where(first_low, i2, i1)
    w_a = jnp.where(first_low, w1, w2)
    w_b = jnp.where(first_low, w2, w1)
    pidx = jnp.where(ea == 0, eb - 1, jnp.where(ea == 1, eb + 1, 5))
    cls = gsel * N_PAIRS + pidx
    cls_ref[0] = cls

    crow = lax.broadcasted_iota(jnp.int32, (CLASS_ROWS, tm), 0)
    oh = crow == cls
    cnt = jnp.dot(jnp.where(oh, 1.0, 0.0).astype(BF16), tri_ref[...], preferred_element_type=F32)
    rank_ref[0] = (jnp.sum(jnp.where(oh, cnt, 0.0), axis=0, keepdims=True) - 1.0).astype(jnp.int32)
    hist_ref[0, 0] = jnp.broadcast_to(cnt[:, tm - 1:tm], (CLASS_ROWS, LANES))

    prow = lax.broadcasted_iota(jnp.int32, (PAY, tm), 0)
    pay = jnp.where(prow == 0, w_a, jnp.where(prow == 1, w_b, 0.0))
    haug_ref[0, :, D:D + PAY] = pay.T


def _outproj(yfox, ysT, x, ada, g_out_fox, w_out, g_norm_ffn, w_rg, b_rg, w_re, b_re, tm):
    B, S, D = x.shape
    nt = S // tm
    wos = w_out[:D_SGU].astype(BF16)
    wof = w_out[D_SGU:].astype(BF16)
    w_re_flat = jnp.transpose(w_re, (1, 0, 2)).reshape(D, N_GROUPS * EXPERTS_PER_GROUP)
    n_log = N_GROUPS + N_GROUPS * EXPERTS_PER_GROUP
    wr = jnp.concatenate([w_rg, w_re_flat, jnp.zeros((D, CLASS_ROWS - n_log), F32)], axis=1).T.astype(BF16)
    br = jnp.concatenate([b_rg, b_re.reshape(-1), jnp.zeros((CLASS_ROWS - n_log,), F32)]).reshape(CLASS_ROWS, 1)
    tri = jnp.triu(jnp.ones((tm, tm), BF16))
    kern = functools.partial(_outproj_kernel, tm=tm, D=D)
    const = lambda shape: pl.BlockSpec(shape, lambda b, j: (0,) * len(shape))
    return pl.pallas_call(
        kern,
        out_shape=(jax.ShapeDtypeStruct((B, S, D), F32),
                   jax.ShapeDtypeStruct((B, S, D + PAY), F32),
                   jax.ShapeDtypeStruct((B, 1, S), jnp.int32),
                   jax.ShapeDtypeStruct((B, 1, S), jnp.int32),
                   jax.ShapeDtypeStruct((B, nt, CLASS_ROWS, LANES), F32)),
        grid=(B, nt),
        in_specs=[pl.BlockSpec((1, tm, D_FOX), lambda b, j: (b, j, 0)),
                  pl.BlockSpec((1, D_SGU, tm), lambda b, j: (b, 0, j)),
                  pl.BlockSpec((1, tm, D), lambda b, j: (b, j, 0)),
                  pl.BlockSpec((1, 6, D), lambda b, j: (b, 0, 0)),
                  const((1, D_FOX)), const((D_FOX, D)), const((D_SGU, D)), const((1, D)),
                  const((CLASS_ROWS, D)), const((CLASS_ROWS, 1)), const((tm, tm))],
        out_specs=(pl.BlockSpec((1, tm, D), lambda b, j: (b, j, 0)),
                   pl.BlockSpec((1, tm, D + PAY), lambda b, j: (b, j, 0)),
                   pl.BlockSpec((1, 1, tm), lambda b, j: (b, 0, j)),
                   pl.BlockSpec((1, 1, tm), lambda b, j: (b, 0, j)),
                   pl.BlockSpec((1, 1, CLASS_ROWS, LANES), lambda b, j: (b, j, 0, 0))),
        compiler_params=_cparams(("arbitrary", "arbitrary")),
        name="outproj",
    )(yfox, ysT, x, ada, g_out_fox.reshape(1, D_FOX), wof, wos, g_norm_ffn.reshape(1, D), wr, br, tri)


def _scatter_kernel(dest_ref, src_ref, init_ref, out_ref, sem, *, tr):
    del init_ref
    base = pl.program_id(0) * tr

    def issue(r, c):
        d = dest_ref[base + r]
        pltpu.make_async_copy(src_ref.at[pl.ds(r, 1), :], out_ref.at[pl.ds(d, 1), :], sem).start()
        return c

    lax.fori_loop(0, tr, issue, 0)
    pltpu.make_async_copy(src_ref, out_ref.at[pl.ds(0, tr), :], sem).wait()


def _scatter(dest, rows, n_pad, tr):
    N, W = rows.shape
    init = jnp.zeros((n_pad, W), rows.dtype)
    kern = functools.partial(_scatter_kernel, tr=tr)
    return pl.pallas_call(
        kern,
        out_shape=jax.ShapeDtypeStruct((n_pad, W), rows.dtype),
        grid_spec=pltpu.PrefetchScalarGridSpec(
            num_scalar_prefetch=1,
            grid=(N // tr,),
            in_specs=[pl.BlockSpec((tr, W), lambda i, d: (i, 0)),
                      pl.BlockSpec(memory_space=pl.ANY)],
            out_specs=pl.BlockSpec(memory_space=pl.ANY),
            scratch_shapes=[pltpu.SemaphoreType.DMA(())]),
        input_output_aliases={2: 0},
        compiler_params=_cparams(("arbitrary",)),
        name="scatter",
    )(dest, rows, init)


def _moe_kernel(e1_ref, e2_ref, valid_ref, h_ref, wg1_ref, wg2_ref, wu1_ref, wu2_ref, wd1_ref, wd2_ref,
                o_ref, *, D):
    t = pl.program_id(0)

    @pl.when(valid_ref[t] == 0)
    def _():
        o_ref[...] = jnp.zeros_like(o_ref)

    @pl.when(valid_ref[t] != 0)
    def _():
        hb = h_ref[:, 0:D].astype(BF16)
        pay = h_ref[:, D:D + PAY]
        y = None
        for wg_ref, wu_ref, wd_ref, lane in ((wg1_ref, wu1_ref, wd1_ref, 0), (wg2_ref, wu2_ref, wd2_ref, 1)):
            g = jnp.dot(hb, wg_ref[0], preferred_element_type=F32)
            u = jnp.dot(hb, wu_ref[0], preferred_element_type=F32)
            a = (g * jax.nn.sigmoid(g)) * u * pay[:, lane:lane + 1]
            d = jnp.dot(a.astype(BF16), wd_ref[0], preferred_element_type=F32)
            y = d if y is None else y + d
        o_ref[...] = y


def _moe(tile_e1, tile_e2, tile_valid, hs, w_gate, w_up, w_down, T):
    n_pad, W = hs.shape
    D = W - PAY
    De = w_gate.shape[2]
    kern = functools.partial(_moe_kernel, D=D)
    return pl.pallas_call(
        kern,
        out_shape=jax.ShapeDtypeStruct((n_pad, D), F32),
        grid_spec=pltpu.PrefetchScalarGridSpec(
            num_scalar_prefetch=3,
            grid=(n_pad // T,),
            in_specs=[pl.BlockSpec((T, W), lambda t, e1, e2, vl: (t, 0)),
                      pl.BlockSpec((1, D, De), lambda t, e1, e2, vl: (e1[t], 0, 0)),
                      pl.BlockSpec((1, D, De), lambda t, e1, e2, vl: (e2[t], 0, 0)),
                      pl.BlockSpec((1, D, De), lambda t, e1, e2, vl: (e1[t], 0, 0)),
                      pl.BlockSpec((1, D, De), lambda t, e1, e2, vl: (e2[t], 0, 0)),
                      pl.BlockSpec((1, De, D), lambda t, e1, e2, vl: (e1[t], 0, 0)),
                      pl.BlockSpec((1, De, D), lambda t, e1, e2, vl: (e2[t], 0, 0))],
            out_specs=pl.BlockSpec((T, D), lambda t, e1, e2, vl: (t, 0))),
        compiler_params=_cparams(("arbitrary",)),
        name="moe",
    )(tile_e1, tile_e2, tile_valid, hs, w_gate, w_gate, w_up, w_up, w_down, w_down)


def _final_kernel(dest_ref, ys_ref, x1_ref, ada_ref, gfin_ref, o_ref, buf, sem, *, tr, nsteps):
    i = pl.program_id(0)

    def issue(step, slot):
        def body(r, c):
            d = dest_ref[step * tr + r]
            pltpu.make_async_copy(ys_ref.at[pl.ds(d, 1), :], buf.at[slot, pl.ds(r, 1), :], sem.at[slot]).start()
            return c
        lax.fori_loop(0, tr, body, 0)

    @pl.when(i == 0)
    def _():
        issue(0, 0)

    @pl.when(i + 1 < nsteps)
    def _():
        issue(i + 1, (i + 1) % 2)

    slot = i % 2
    pltpu.make_async_copy(ys_ref.at[pl.ds(0, tr), :], buf.at[slot], sem.at[slot]).wait()
    gt2 = ada_ref[0][5:6]
    xo = x1_ref[...] + gt2 * buf[slot]
    ms = jnp.mean(xo * xo, axis=-1, keepdims=True)
    o_ref[...] = xo * lax.rsqrt(ms + EPS) * gfin_ref[...]


def _final(dest, ys, x1, ada, g_final, S, tr):
    N, D = x1.shape
    nsteps = N // tr
    per_b = S // tr
    kern = functools.partial(_final_kernel, tr=tr, nsteps=nsteps)
    return pl.pallas_call(
        kern,
        out_shape=jax.ShapeDtypeStruct((N, D), F32),
        grid_spec=pltpu.PrefetchScalarGridSpec(
            num_scalar_prefetch=1,
            grid=(nsteps,),
            in_specs=[pl.BlockSpec(memory_space=pl.ANY),
                      pl.BlockSpec((tr, D), lambda i, d: (i, 0)),
                      pl.BlockSpec((1, 6, D), lambda i, d: (i // per_b, 0, 0)),
                      pl.BlockSpec((1, D), lambda i, d: (0, 0))],
            out_specs=pl.BlockSpec((tr, D), lambda i, d: (i, 0)),
            scratch_shapes=[pltpu.VMEM((2, tr, D), F32), pltpu.SemaphoreType.DMA((2,))]),
        compiler_params=_cparams(("arbitrary",)),
        name="final",
    )(dest, ys, x1, ada, g_final.reshape(1, D))


def _tile_sizes(S):
    tm = min(512, S)
    tq = min(256, S)
    return tm, tq


def _layer(x, ada, g_norm_mix, w_in, g_sgu, w_spatial, b_spatial, b_forget, g_out_sgu, g_out_fox, w_out,
           g_norm_ffn, w_rg, b_rg, w_re, b_re, w_gate, w_up, w_down, g_final):
    B, S, D = x.shape
    N = B * S
    tm, tq = _tile_sizes(S)
    T = min(256, N)
    q, kT, v, ysT, Fl2 = _inproj(x, ada, g_norm_mix, w_in, g_sgu, w_spatial, b_spatial, b_forget, g_out_sgu, tm)
    yfox = _attention(q, kT, v, Fl2, tq)
    x1, haug, cls, rank, hist = _outproj(yfox, ysT, x, ada, g_out_fox, w_out, g_norm_ffn, w_rg, b_rg, w_re, b_re, tm)

    counts = hist[:, :, :N_CLASSES, 0].reshape(-1, N_CLASSES).astype(jnp.int32)
    tot = counts.sum(axis=0)
    padded = ((tot + T - 1) // T) * T
    ends = jnp.cumsum(padded)
    class_off = ends - padded
    base = class_off[None, :] + jnp.cumsum(counts, axis=0) - counts
    clsf = cls.reshape(N)
    tile_of = jnp.arange(N, dtype=jnp.int32) // tm
    onehot = clsf[:, None] == jnp.arange(N_CLASSES, dtype=jnp.int32)[None, :]
    dest = jnp.sum(jnp.where(onehot, base[tile_of], 0), axis=1).astype(jnp.int32) + rank.reshape(N)
    n_tiles = N // T + N_CLASSES
    n_pad = n_tiles * T
    tstart = jnp.arange(n_tiles, dtype=jnp.int32) * T
    tcls = jnp.minimum(jnp.sum(tstart[:, None] >= ends[None, :], axis=1), N_CLASSES - 1).astype(jnp.int32)
    tile_valid = (tstart < ends[-1]).astype(jnp.int32)
    grp, pidx = tcls // N_PAIRS, tcls % N_PAIRS
    tile_e1 = grp * EXPERTS_PER_GROUP + jnp.asarray(PAIR_A, jnp.int32)[pidx]
    tile_e2 = grp * EXPERTS_PER_GROUP + jnp.asarray(PAIR_B, jnp.int32)[pidx]

    hs = _scatter(dest, haug.reshape(N, D + PAY), n_pad, min(512, N))
    ys = _moe(tile_e1, tile_e2, tile_valid, hs, w_gate.astype(BF16), w_up.astype(BF16), w_down.astype(BF16), T)
    out = _final(dest, ys, x1.reshape(N, D), ada, g_final, S, min(512, S))
    return out.reshape(B, S, D)


def kernel(x, c, w_ada, b_ada, g_norm_mix, w_in, g_sgu, w_spatial, b_spatial, b_forget, g_out_sgu, g_out_fox, w_out, g_norm_ffn, w_router_group, b_router_group, w_router_expert, b_router_expert, w_gate, w_up, w_down, g_final):
    assert w_ada.shape[0] == 1, "single-layer trunk"
    ada = _ada(c, w_ada[0], b_ada[0])
    return _layer(x, ada, g_norm_mix[0], w_in[0], g_sgu[0], w_spatial[0], b_spatial[0], b_forget[0],
                  g_out_sgu[0], g_out_fox[0], w_out[0], g_norm_ffn[0], w_router_group[0], b_router_group[0],
                  w_router_expert[0], b_router_expert[0], w_gate[0], w_up[0], w_down[0], g_final)
```

```python
import functools
import math

import jax
import jax.numpy as jnp
from jax import lax
from jax.experimental import pallas as pl
from jax.experimental.pallas import tpu as pltpu

F32 = jnp.float32
BF16 = jnp.bfloat16

HEAD_DIM = 64
N_HEADS_SGU = 8
N_HEADS_FOX = 8
D_SGU = N_HEADS_SGU * HEAD_DIM
D_FOX = N_HEADS_FOX * HEAD_DIM
CHUNK = 128
N_GROUPS = 4
EXPERTS_PER_GROUP = 4
N_PAIRS = 6
N_CLASSES = N_GROUPS * N_PAIRS
CLASS_ROWS = 32
EPS = 1e-6
LOG2E = math.log2(math.e)
LANES = 128
PAY = LANES
F_ROWS = 16

VMEM_LIMIT = 56 * 1024 * 1024

PAIR_A = (0, 0, 0, 1, 1, 2)
PAIR_B = (1, 2, 3, 2, 3, 3)


def _cparams(sem):
    return pltpu.CompilerParams(dimension_semantics=sem, vmem_limit_bytes=VMEM_LIMIT)


def _ada_kernel(c_ref, w_ref, b_ref, o_ref):
    c = c_ref[...]
    ca = (c * jax.nn.sigmoid(c)).astype(BF16)
    o_ref[...] = jnp.dot(ca, w_ref[...].astype(BF16), preferred_element_type=F32) + b_ref[...]


def _ada(c, w_ada, b_ada):
    B, D = c.shape
    n6 = w_ada.shape[1]
    rows = 8
    cp = jnp.zeros((rows, D), F32).at[:B].set(c)
    tn = 1536
    out = pl.pallas_call(
        _ada_kernel,
        out_shape=jax.ShapeDtypeStruct((rows, n6), F32),
        grid=(n6 // tn,),
        in_specs=[pl.BlockSpec((rows, D), lambda j: (0, 0)),
                  pl.BlockSpec((D, tn), lambda j: (0, j)),
                  pl.BlockSpec((1, tn), lambda j: (0, j))],
        out_specs=pl.BlockSpec((rows, tn), lambda j: (0, j)),
        compiler_params=_cparams(("arbitrary",)),
        name="ada",
    )(cp, w_ada, b_ada.reshape(1, n6))
    return out[:B].reshape(B, 6, D)


def _log_sigmoid(x):
    return jnp.minimum(x, 0.0) - jnp.log1p(jnp.exp(-jnp.abs(x)))


def _inproj_kernel(x_ref, ada_ref, gmix_ref, wt_ref, wn_ref, gsgu_ref, ws_ref, bsp_ref, bf_ref, gos_ref,
                   q_ref, kT_ref, v_ref, ysT_ref, F_ref, carry_ref, *, tm):
    j = pl.program_id(1)
    x = x_ref[0]
    ada = ada_ref[0]
    sh1, sc1 = ada[0:1], ada[1:2]
    ms = jnp.mean(x * x, axis=-1, keepdims=True)
    h = (x * lax.rsqrt(ms + EPS) * gmix_ref[...]) * (1.0 + sc1) + sh1
    hb = h.astype(BF16)
    pT = lax.dot_general(wt_ref[...], hb, (((1,), (1,)), ((), ())), preferred_element_type=F32)
    pN = jnp.dot(hb, wn_ref[...], preferred_element_type=F32)
    q_ref[0] = (pN[:, :D_FOX] * (LOG2E / math.sqrt(HEAD_DIM))).astype(BF16)
    v_ref[0] = pN[:, D_FOX:].astype(BF16)
    kT_ref[0] = pT[2 * D_SGU:2 * D_SGU + D_FOX].astype(BF16)

    uT = jax.nn.gelu(pT[0:D_SGU])
    v3 = jax.nn.gelu(pT[D_SGU:2 * D_SGU]).reshape(N_HEADS_SGU, HEAD_DIM, tm)
    msv = jnp.mean(v3 * v3, axis=1, keepdims=True)
    vn = ((v3 * lax.rsqrt(msv + EPS)).reshape(D_SGU, tm) * gsgu_ref[...]).astype(BF16)
    srow = lax.broadcasted_iota(jnp.int32, (CHUNK, CHUNK), 0)
    tcol = lax.broadcasted_iota(jnp.int32, (CHUNK, CHUNK), 1)
    causal = srow <= tcol
    rows = []
    for hd in range(N_HEADS_SGU):
        w = jnp.where(causal, ws_ref[hd], jnp.zeros((), BF16))
        cols = [jnp.dot(vn[hd * HEAD_DIM:(hd + 1) * HEAD_DIM, c * CHUNK:(c + 1) * CHUNK], w,
                        preferred_element_type=F32) + bsp_ref[hd * HEAD_DIM:(hd + 1) * HEAD_DIM, :]
                for c in range(tm // CHUNK)]
        rows.append(jnp.concatenate(cols, axis=1))
    y = uT * jnp.concatenate(rows, axis=0)
    msy = jnp.mean(y * y, axis=0, keepdims=True)
    ysT_ref[0] = (y * lax.rsqrt(msy + EPS) * gos_ref[...]).astype(BF16)

    lf = _log_sigmoid(pT[2 * D_SGU + D_FOX:2 * D_SGU + D_FOX + N_HEADS_FOX] + bf_ref[...])
    lane = lax.broadcasted_iota(jnp.int32, lf.shape, 1)
    sft = 1
    while sft < tm:
        lf = lf + jnp.where(lane >= sft, pltpu.roll(lf, sft, axis=1), 0.0)
        sft *= 2

    @pl.when(j == 0)
    def _():
        carry_ref[...] = jnp.zeros_like(carry_ref)

    Fc = lf + carry_ref[:, 0:1]
    carry_ref[...] = jnp.broadcast_to(Fc[:, tm - 1:tm], carry_ref.shape)
    F_ref[0] = Fc * LOG2E


def _inproj(x, ada, g_norm_mix, w_in, g_sgu, w_spatial, b_spatial, b_forget, g_out_sgu, tm):
    B, S, D = x.shape
    o_u, o_v, o_q, o_k, o_vf, o_f = 0, D_SGU, 2 * D_SGU, 2 * D_SGU + D_FOX, 2 * D_SGU + 2 * D_FOX, 2 * D_SGU + 3 * D_FOX
    wT = jnp.concatenate([w_in[:, o_u:o_q], w_in[:, o_k:o_vf], w_in[:, o_f:],
                          jnp.zeros((D, F_ROWS - N_HEADS_FOX), F32)], axis=1).T.astype(BF16)
    wN = jnp.concatenate([w_in[:, o_q:o_k], w_in[:, o_vf:o_f]], axis=1).astype(BF16)
    nT = wT.shape[0]
    wsT = jnp.swapaxes(w_spatial, 1, 2).astype(BF16)
    bsp = jnp.repeat(b_spatial, HEAD_DIM, axis=0)
    kern = functools.partial(_inproj_kernel, tm=tm)
    const = lambda shape: pl.BlockSpec(shape, lambda b, j: (0,) * len(shape))
    return pl.pallas_call(
        kern,
        out_shape=(jax.ShapeDtypeStruct((B, S, D_FOX), BF16),
                   jax.ShapeDtypeStruct((B, D_FOX, S), BF16),
                   jax.ShapeDtypeStruct((B, S, D_FOX), BF16),
                   jax.ShapeDtypeStruct((B, D_SGU, S), BF16),
                   jax.ShapeDtypeStruct((B, N_HEADS_FOX, S), F32)),
        grid=(B, S // tm),
        in_specs=[pl.BlockSpec((1, tm, D), lambda b, j: (b, j, 0)),
                  pl.BlockSpec((1, 6, D), lambda b, j: (b, 0, 0)),
                  const((1, D)), const((nT, D)), const((D, 2 * D_FOX)), const((D_SGU, 1)),
                  const((N_HEADS_SGU, CHUNK, CHUNK)), const((D_SGU, CHUNK)), const((N_HEADS_FOX, 1)),
                  const((D_SGU, 1))],
        out_specs=(pl.BlockSpec((1, tm, D_FOX), lambda b, j: (b, j, 0)),
                   pl.BlockSpec((1, D_FOX, tm), lambda b, j: (b, 0, j)),
                   pl.BlockSpec((1, tm, D_FOX), lambda b, j: (b, j, 0)),
                   pl.BlockSpec((1, D_SGU, tm), lambda b, j: (b, 0, j)),
                   pl.BlockSpec((1, N_HEADS_FOX, tm), lambda b, j: (b, 0, j))),
        scratch_shapes=[pltpu.VMEM((N_HEADS_FOX, LANES), F32)],
        compiler_params=_cparams(("arbitrary", "arbitrary")),
        name="inproj",
    )(x, ada, g_norm_mix.reshape(1, D), wT, wN, g_sgu.reshape(D_SGU, 1), wsT, bsp,
      b_forget.reshape(N_HEADS_FOX, 1), g_out_sgu.reshape(D_SGU, 1))


def _attn_kernel(q_ref, kT_ref, v_ref, F_ref, o_ref, m_scr, l_scr, acc_scr, *, tq, S):
    pair = pl.program_id(1)
    tk = tq
    rep = tk // LANES
    lane = lax.broadcasted_iota(jnp.int32, (tq, LANES), 1)
    row = lax.broadcasted_iota(jnp.int32, (tq, tk), 0)
    col = lax.broadcasted_iota(jnp.int32, (tq, tk), 1)

    def q_body(qi, carry):
        q = q_ref[0, pl.ds(pl.multiple_of(qi * tq, tq), tq), :]
        zero = jnp.zeros((), BF16)
        qh = (jnp.where(lane < HEAD_DIM, q, zero), jnp.where(lane >= HEAD_DIM, q, zero))
        m_scr[...] = jnp.full(m_scr.shape, -jnp.inf, F32)
        l_scr[...] = jnp.zeros(l_scr.shape, F32)
        acc_scr[...] = jnp.zeros(acc_scr.shape, F32)

        def k_step(kj, diag):
            koff = pl.multiple_of(kj * tk, tk)
            kT = kT_ref[0, :, pl.ds(koff, tk)]
            vv = v_ref[0, pl.ds(koff, tk), :]
            for hh in range(2):
                s = jnp.dot(qh[hh], kT, preferred_element_type=F32)
                s = s - F_ref[0, pl.ds(2 * pair + hh, 1), pl.ds(koff, tk)]
                if diag:
                    s = jnp.where(col <= row, s, -jnp.inf)
                m_prev = m_scr[hh]
                m_new = jnp.maximum(m_prev, jnp.max(s, axis=1, keepdims=True))
                alpha = jnp.exp2(m_prev - m_new)
                p = jnp.exp2(s - jnp.concatenate([m_new] * rep, axis=1))
                l_scr[hh] = alpha * l_scr[hh] + jnp.sum(p, axis=1, keepdims=True)
                acc_scr[hh] = alpha * acc_scr[hh] + jnp.dot(p.astype(BF16), vv, preferred_element_type=F32)
                m_scr[hh] = m_new

        def k_body(kj, c):
            k_step(kj, False)
            return c

        lax.fori_loop(0, qi, k_body, 0)
        k_step(qi, True)
        o = jnp.where(lane < HEAD_DIM, acc_scr[0] / l_scr[0], acc_scr[1] / l_scr[1])
        o_ref[0, pl.ds(pl.multiple_of(qi * tq, tq), tq), :] = o.astype(o_ref.dtype)
        return carry

    lax.fori_loop(0, S // tq, q_body, 0)


def _attention(q, kT, v, Fl2, tq):
    B, S, _ = q.shape
    kern = functools.partial(_attn_kernel, tq=tq, S=S)
    return pl.pallas_call(
        kern,
        out_shape=jax.ShapeDtypeStruct((B, S, D_FOX), BF16),
        grid=(B, N_HEADS_FOX // 2),
        in_specs=[pl.BlockSpec((1, S, LANES), lambda b, p: (b, 0, p)),
                  pl.BlockSpec((1, LANES, S), lambda b, p: (b, p, 0)),
                  pl.BlockSpec((1, S, LANES), lambda b, p: (b, 0, p)),
                  pl.BlockSpec((1, N_HEADS_FOX, S), lambda b, p: (b, 0, 0))],
        out_specs=pl.BlockSpec((1, S, LANES), lambda b, p: (b, 0, p)),
        scratch_shapes=[pltpu.VMEM((2, tq, LANES), F32), pltpu.VMEM((2, tq, LANES), F32),
                        pltpu.VMEM((2, tq, LANES), F32)],
        compiler_params=_cparams(("arbitrary", "arbitrary")),
        name="attn",
    )(q, kT, v, Fl2)


def _first_argmax(vals):
    best, idx = vals[0], jnp.zeros(vals[0].shape, jnp.int32)
    for i in range(1, len(vals)):
        upd = vals[i] > best
        idx = jnp.where(upd, i, idx)
        best = jnp.where(upd, vals[i], best)
    return best, idx


def _outproj_kernel(yf_ref, ysT_ref, x_ref, ada_ref, gof_ref, wof_ref, wos_ref, gffn_ref, wr_ref, br_ref,
                    tri_ref, x1_ref, haug_ref, cls_ref, rank_ref, hist_ref, *, tm, D):
    ada = ada_ref[0]
    gt1, sh2, sc2 = ada[2:3], ada[3:4], ada[4:5]
    yf = yf_ref[0].astype(F32)
    msf = jnp.mean(yf * yf, axis=-1, keepdims=True)
    yfn = (yf * lax.rsqrt(msf + EPS) * gof_ref[...]).astype(BF16)
    mix = jnp.dot(yfn, wof_ref[...], preferred_element_type=F32)
    mix = mix + lax.dot_general(ysT_ref[0], wos_ref[...], (((0,), (0,)), ((), ())),
                                preferred_element_type=F32)
    x1 = x_ref[0] + gt1 * mix
    x1_ref[0] = x1
    ms = jnp.mean(x1 * x1, axis=-1, keepdims=True)
    h2 = (x1 * lax.rsqrt(ms + EPS) * gffn_ref[...]) * (1.0 + sc2) + sh2
    haug_ref[0, :, 0:D] = h2

    lg = lax.dot_general(wr_ref[...], h2.astype(BF16), (((1,), (1,)), ((), ())),
                         preferred_element_type=F32) + br_ref[...]
    gl = [lg[i:i + 1] for i in range(N_GROUPS)]
    gmax, gsel = _first_argmax(gl)
    denom = gl[0] * 0.0
    for i in range(N_GROUPS):
        denom = denom + jnp.exp(gl[i] - gmax)
    p_sel = 1.0 / denom
    le = []
    for e in range(EXPERTS_PER_GROUP):
        r = lg[N_GROUPS + e:N_GROUPS + e + 1]
        for g in range(1, N_GROUPS):
            r = jnp.where(gsel == g, lg[N_GROUPS + g * EXPERTS_PER_GROUP + e:N_GROUPS + g * EXPERTS_PER_GROUP + e + 1], r)
        le.append(r)
    v1, i1 = _first_argmax(le)
    v2, i2 = _first_argmax([jnp.where(i1 == e, -jnp.inf, le[e]) for e in range(EXPERTS_PER_GROUP)])
    e21 = jnp.exp(v2 - v1)
    w1 = p_sel * (1.0 / (1.0 + e21))
    w2 = p_sel * (e21 / (1.0 + e21))
    first_low = i1 < i2
    ea = jnp.where(first_low, i1, i2)
    eb = jnp.where(first_low, i2, i1)
    w_a = jnp.where(first_low, w1, w2)
    w_b = jnp.where(first_low, w2, w1)
    pidx = jnp.where(ea == 0, eb - 1, jnp.where(ea == 1, eb + 1, 5))
    cls = gsel * N_PAIRS + pidx
    cls_ref[0] = cls

    crow = lax.broadcasted_iota(jnp.int32, (CLASS_ROWS, tm), 0)
    oh = crow == cls
    cnt = jnp.dot(jnp.where(oh, 1.0, 0.0).astype(BF16), tri_ref[...], preferred_element_type=F32)
    rank_ref[0] = (jnp.sum(jnp.where(oh, cnt, 0.0), axis=0, keepdims=True) - 1.0).astype(jnp.int32)
    hist_ref[0, 0] = jnp.broadcast_to(cnt[:, tm - 1:tm], (CLASS_ROWS, LANES))

    prow = lax.broadcasted_iota(jnp.int32, (PAY, tm), 0)
    pay = jnp.where(prow == 0, w_a, jnp.where(prow == 1, w_b, 0.0))
    haug_ref[0, :, D:D + PAY] = pay.T


def _outproj(yfox, ysT, x, ada, g_out_fox, w_out, g_norm_ffn, w_rg, b_rg, w_re, b_re, tm):
    B, S, D = x.shape
    nt = S // tm
    wos = w_out[:D_SGU].astype(BF16)
    wof = w_out[D_SGU:].astype(BF16)
    w_re_flat = jnp.transpose(w_re, (1, 0, 2)).reshape(D, N_GROUPS * EXPERTS_PER_GROUP)
    n_log = N_GROUPS + N_GROUPS * EXPERTS_PER_GROUP
    wr = jnp.concatenate([w_rg, w_re_flat, jnp.zeros((D, CLASS_ROWS - n_log), F32)], axis=1).T.astype(BF16)
    br = jnp.concatenate([b_rg, b_re.reshape(-1), jnp.zeros((CLASS_ROWS - n_log,), F32)]).reshape(CLASS_ROWS, 1)
    tri = jnp.triu(jnp.ones((tm, tm), BF16))
    kern = functools.partial(_outproj_kernel, tm=tm, D=D)
    const = lambda shape: pl.BlockSpec(shape, lambda b, j: (0,) * len(shape))
    return pl.pallas_call(
        kern,
        out_shape=(jax.ShapeDtypeStruct((B, S, D), F32),
                   jax.ShapeDtypeStruct((B, S, D + PAY), F32),
                   jax.ShapeDtypeStruct((B, 1, S), jnp.int32),
                   jax.ShapeDtypeStruct((B, 1, S), jnp.int32),
                   jax.ShapeDtypeStruct((B, nt, CLASS_ROWS, LANES), F32)),
        grid=(B, nt),
        in_specs=[pl.BlockSpec((1, tm, D_FOX), lambda b, j: (b, j, 0)),
                  pl.BlockSpec((1, D_SGU, tm), lambda b, j: (b, 0, j)),
                  pl.BlockSpec((1, tm, D), lambda b, j: (b, j, 0)),
                  pl.BlockSpec((1, 6, D), lambda b, j: (b, 0, 0)),
                  const((1, D_FOX)), const((D_FOX, D)), const((D_SGU, D)), const((1, D)),
                  const((CLASS_ROWS, D)), const((CLASS_ROWS, 1)), const((tm, tm))],
        out_specs=(pl.BlockSpec((1, tm, D), lambda b, j: (b, j, 0)),
                   pl.BlockSpec((1, tm, D + PAY), lambda b, j: (b, j, 0)),
                   pl.BlockSpec((1, 1, tm), lambda b, j: (b, 0, j)),
                   pl.BlockSpec((1, 1, tm), lambda b, j: (b, 0, j)),
                   pl.BlockSpec((1, 1, CLASS_ROWS, LANES), lambda b, j: (b, j, 0, 0))),
        compiler_params=_cparams(("arbitrary", "arbitrary")),
        name="outproj",
    )(yfox, ysT, x, ada, g_out_fox.reshape(1, D_FOX), wof, wos, g_norm_ffn.reshape(1, D), wr, br, tri)


def _scatter_kernel(dest_ref, src_ref, init_ref, out_ref, sem, *, tr):
    del init_ref
    base = pl.program_id(0) * tr

    def issue(r, c):
        d = dest_ref[base + r]
        pltpu.make_async_copy(src_ref.at[pl.ds(r, 1), :], out_ref.at[pl.ds(d, 1), :], sem).start()
        return c

    lax.fori_loop(0, tr, issue, 0)
    pltpu.make_async_copy(src_ref, out_ref.at[pl.ds(0, tr), :], sem).wait()


def _scatter(dest, rows, n_pad, tr):
    N, W = rows.shape
    init = jnp.zeros((n_pad, W), rows.dtype)
    kern = functools.partial(_scatter_kernel, tr=tr)
    return pl.pallas_call(
        kern,
        out_shape=jax.ShapeDtypeStruct((n_pad, W), rows.dtype),
        grid_spec=pltpu.PrefetchScalarGridSpec(
            num_scalar_prefetch=1,
            grid=(N // tr,),
            in_specs=[pl.BlockSpec((tr, W), lambda i, d: (i, 0)),
                      pl.BlockSpec(memory_space=pl.ANY)],
            out_specs=pl.BlockSpec(memory_space=pl.ANY),
            scratch_shapes=[pltpu.SemaphoreType.DMA(())]),
        input_output_aliases={2: 0},
        compiler_params=_cparams(("arbitrary",)),
        name="scatter",
    )(dest, rows, init)


def _moe_kernel(e1_ref, e2_ref, valid_ref, h_ref, wg1_ref, wg2_ref, wu1_ref, wu2_ref, wd1_ref, wd2_ref,
                o_ref, *, D):
    t = pl.program_id(0)

    @pl.when(valid_ref[t] == 0)
    def _():
        o_ref[...] = jnp.zeros_like(o_ref)

    @pl.when(valid_ref[t] != 0)
    def _():
        hb = h_ref[:, 0:D].astype(BF16)
        pay = h_ref[:, D:D + PAY]
        y = None
        for wg_ref, wu_ref, wd_ref, lane in ((wg1_ref, wu1_ref, wd1_ref, 0), (wg2_ref, wu2_ref, wd2_ref, 1)):
            g = jnp.dot(hb, wg_ref[0], preferred_element_type=F32)
            u = jnp.dot(hb, wu_ref[0], preferred_element_type=F32)
            a = (g * jax.nn.sigmoid(g)) * u * pay[:, lane:lane + 1]
            d = jnp.dot(a.astype(BF16), wd_ref[0], preferred_element_type=F32)
            y = d if y is None else y + d
        o_ref[...] = y


def _moe(tile_e1, tile_e2, tile_valid, hs, w_gate, w_up, w_down, T):
    n_pad, W = hs.shape
    D = W - PAY
    De = w_gate.shape[2]
    kern = functools.partial(_moe_kernel, D=D)
    return pl.pallas_call(
        kern,
        out_shape=jax.ShapeDtypeStruct((n_pad, D), F32),
        grid_spec=pltpu.PrefetchScalarGridSpec(
            num_scalar_prefetch=3,
            grid=(n_pad // T,),
            in_specs=[pl.BlockSpec((T, W), lambda t, e1, e2, vl: (t, 0)),
                      pl.BlockSpec((1, D, De), lambda t, e1, e2, vl: (e1[t], 0, 0)),
                      pl.BlockSpec((1, D, De), lambda t, e1, e2, vl: (e2[t], 0, 0)),
                      pl.BlockSpec((1, D, De), lambda t, e1, e2, vl: (e1[t], 0, 0)),
                      pl.BlockSpec((1, D, De), lambda t, e1, e2, vl: (e2[t], 0, 0)),
                      pl.BlockSpec((1, De, D), lambda t, e1, e2, vl: (e1[t], 0, 0)),
                      pl.BlockSpec((1, De, D), lambda t, e1, e2, vl: (e2[t], 0, 0))],
            out_specs=pl.BlockSpec((T, D), lambda t, e1, e2, vl: (t, 0))),
        compiler_params=_cparams(("arbitrary",)),
        name="moe",
    )(tile_e1, tile_e2, tile_valid, hs, w_gate, w_gate, w_up, w_up, w_down, w_down)


def _final_kernel(dest_ref, ys_ref, x1_ref, ada_ref, gfin_ref, o_ref, buf, sem, *, tr, nsteps):
    i = pl.program_id(0)

    def issue(step, slot):
        def body(r, c):
            d = dest_ref[step * tr + r]
            pltpu.make_async_copy(ys_ref.at[pl.ds(d, 1), :], buf.at[slot, pl.ds(r, 1), :], sem.at[slot]).start()
            return c
        lax.fori_loop(0, tr, body, 0)

    @pl.when(i == 0)
    def _():
        issue(0, 0)

    @pl.when(i + 1 < nsteps)
    def _():
        issue(i + 1, (i + 1) % 2)

    slot = i % 2
    pltpu.make_async_copy(ys_ref.at[pl.ds(0, tr), :], buf.at[slot], sem.at[slot]).wait()
    gt2 = ada_ref[0][5:6]
    xo = x1_ref[...] + gt2 * buf[slot]
    ms = jnp.mean(xo * xo, axis=-1, keepdims=True)
    o_ref[...] = xo * lax.rsqrt(ms + EPS) * gfin_ref[...]


def _final(dest, ys, x1, ada, g_final, S, tr):
    N, D = x1.shape
    nsteps = N // tr
    per_b = S // tr
    kern = functools.partial(_final_kernel, tr=tr, nsteps=nsteps)
    return pl.pallas_call(
        kern,
        out_shape=jax.ShapeDtypeStruct((N, D), F32),
        grid_spec=pltpu.PrefetchScalarGridSpec(
            num_scalar_prefetch=1,
            grid=(nsteps,),
            in_specs=[pl.BlockSpec(memory_space=pl.ANY),
                      pl.BlockSpec((tr, D), lambda i, d: (i, 0)),
                      pl.BlockSpec((1, 6, D), lambda i, d: (i // per_b, 0, 0)),
                      pl.BlockSpec((1, D), lambda i, d: (0, 0))],
            out_specs=pl.BlockSpec((tr, D), lambda i, d: (i, 0)),
            scratch_shapes=[pltpu.VMEM((2, tr, D), F32), pltpu.SemaphoreType.DMA((2,))]),
        compiler_params=_cparams(("arbitrary",)),
        name="final",
    )(dest, ys, x1, ada, g_final.reshape(1, D))


def _tile_sizes(S):
    tm = min(512, S)
    tq = min(512, S)
    return tm, tq


def _layer(x, ada, g_norm_mix, w_in, g_sgu, w_spatial, b_spatial, b_forget, g_out_sgu, g_out_fox, w_out,
           g_norm_ffn, w_rg, b_rg, w_re, b_re, w_gate, w_up, w_down, g_final):
    B, S, D = x.shape
    N = B * S
    tm, tq = _tile_sizes(S)
    T = min(256, N)
    q, kT, v, ysT, Fl2 = _inproj(x, ada, g_norm_mix, w_in, g_sgu, w_spatial, b_spatial, b_forget, g_out_sgu, tm)
    yfox = _attention(q, kT, v, Fl2, tq)
    x1, haug, cls, rank, hist = _outproj(yfox, ysT, x, ada, g_out_fox, w_out, g_norm_ffn, w_rg, b_rg, w_re, b_re, tm)

    counts = hist[:, :, :N_CLASSES, 0].reshape(-1, N_CLASSES).astype(jnp.int32)
    tot = counts.sum(axis=0)
    padded = ((tot + T - 1) // T) * T
    ends = jnp.cumsum(padded)
    class_off = ends - padded
    base = class_off[None, :] + jnp.cumsum(counts, axis=0) - counts
    clsf = cls.reshape(N)
    tile_of = jnp.arange(N, dtype=jnp.int32) // tm
    onehot = clsf[:, None] == jnp.arange(N_CLASSES, dtype=jnp.int32)[None, :]
    dest = jnp.sum(jnp.where(onehot, base[tile_of], 0), axis=1).astype(jnp.int32) + rank.reshape(N)
    n_tiles = N // T + N_CLASSES
    n_pad = n_tiles * T
    tstart = jnp.arange(n_tiles, dtype=jnp.int32) * T
    tcls = jnp.minimum(jnp.sum(tstart[:, None] >= ends[None, :], axis=1), N_CLASSES - 1).astype(jnp.int32)
    tile_valid = (tstart < ends[-1]).astype(jnp.int32)
    grp, pidx = tcls // N_PAIRS, tcls % N_PAIRS
    tile_e1 = grp * EXPERTS_PER_GROUP + jnp.asarray(PAIR_A, jnp.int32)[pidx]
    tile_e2 = grp * EXPERTS_PER_GROUP + jnp.asarray(PAIR_B, jnp.int32)[pidx]

    hs = _scatter(dest, haug.reshape(N, D + PAY), n_pad, min(512, N))
    ys = _moe(tile_e1, tile_e2, tile_valid, hs, w_gate.astype(BF16), w_up.astype(BF16), w_down.astype(BF16), T)
    out = _final(dest, ys, x1.reshape(N, D), ada, g_final, S, min(512, S))
    return out.reshape(B, S, D)


def kernel(x, c, w_ada, b_ada, g_norm_mix, w_in, g_sgu, w_spatial, b_spatial, b_forget, g_out_sgu, g_out_fox, w_out, g_norm_ffn, w_router_group, b_router_group, w_router_expert, b_router_expert, w_gate, w_up, w_down, g_final):
    assert w_ada.shape[0] == 1, "single-layer trunk"
    ada = _ada(c, w_ada[0], b_ada[0])
    return _layer(x, ada, g_norm_mix[0], w_in[0], g_sgu[0], w_spatial[0], b_spatial[0], b_forget[0],
                  g_out_sgu[0], g_out_fox[0], w_out[0], g_norm_ffn[0], w_router_group[0], b_router_group[0],
                  w_router_expert[0], b_router_expert[0], w_gate[0], w_up[0], w_down[0], g_final)
```
